```python
import jax, jax.numpy as jnp
from jax import lax
import numpy as np

D_MODEL = 1024
BATCH = 8
SEQ = 8192
DEPTH = 1

CHUNK = 64
D_SHORT = D_MODEL
D_CONF = D_MODEL
SHORT_K = 3
CONF_K = 31
D_FF = -(-8 * D_MODEL // (3 * 256)) * 256
N_MOD = 6
EPS = 1e-6
LN_EPS = 1e-5
SPLITS = (D_SHORT, 2 * D_SHORT, 3 * D_SHORT,
          3 * D_SHORT + D_CONF, 3 * D_SHORT + 2 * D_CONF,
          3 * D_SHORT + 2 * D_CONF + D_MODEL)
D_IN = 3 * D_SHORT + 2 * D_CONF + 2 * D_MODEL

kernel_name = 'hybrid_shortconv_conformer_gated_block'


def rms_norm(x, g):
    xf = x.astype(jnp.float32)
    y = xf * lax.rsqrt(jnp.mean(xf * xf, axis=-1, keepdims=True) + EPS)
    return (y * g.astype(jnp.float32)).astype(x.dtype)


def layer_norm(x, g, b):
    xf = x.astype(jnp.float32)
    mu = jnp.mean(xf, axis=-1, keepdims=True)
    var = jnp.mean(jnp.square(xf - mu), axis=-1, keepdims=True)
    y = (xf - mu) * lax.rsqrt(var + LN_EPS)
    return (y * g.astype(jnp.float32) + b.astype(jnp.float32)).astype(x.dtype)


def causal_dwconv(u, w):
    k, ch = w.shape
    return lax.conv_general_dilated(
        u, w.astype(u.dtype)[:, None, :], window_strides=(1,), padding=[(k - 1, 0)],
        dimension_numbers=('NWC', 'WIO', 'NWC'), feature_group_count=ch)


def setup_inputs(seed: int = 0) -> dict:
    key = jax.random.key(seed)
    ks = jax.random.split(key, 20)
    n = jax.random.normal
    d = D_MODEL
    return {
        'x': n(ks[0], (BATCH, SEQ, d), jnp.float32),
        'c': n(ks[1], (BATCH, d), jnp.float32),
        'w_ada': n(ks[2], (DEPTH, d, N_MOD * d), jnp.float32) * (0.5 * d ** -0.5),
        'b_ada': n(ks[3], (DEPTH, N_MOD * d), jnp.float32) * 0.02,
        'norm_mix_g': 1.0 + 0.05 * n(ks[4], (DEPTH, d), jnp.float32),
        'w_in': n(ks[5], (DEPTH, d, D_IN), jnp.float32) * d ** -0.5,
        'conv_short_w': n(ks[6], (DEPTH, SHORT_K, D_SHORT), jnp.float32) * SHORT_K ** -0.5,
        'w_short_out': n(ks[7], (DEPTH, D_SHORT, d), jnp.float32) * D_SHORT ** -0.5,
        'conv_conf_w': n(ks[8], (DEPTH, CONF_K, D_CONF), jnp.float32) * CONF_K ** -0.5,
        'conv_conf_b': n(ks[9], (DEPTH, D_CONF), jnp.float32) * 0.02,
        'conf_ln_g': 1.0 + 0.05 * n(ks[10], (DEPTH, D_CONF), jnp.float32),
        'conf_ln_b': n(ks[11], (DEPTH, D_CONF), jnp.float32) * 0.02,
        'w_conf_out': n(ks[12], (DEPTH, D_CONF, d), jnp.float32) * D_CONF ** -0.5,
        'w_o': n(ks[13], (DEPTH, d, d), jnp.float32) * d ** -0.5,
        'norm_ffn_g': 1.0 + 0.05 * n(ks[14], (DEPTH, d), jnp.float32),
        'w_ffn_in': n(ks[15], (DEPTH, d, 2 * D_FF), jnp.float32) * d ** -0.5,
        'w_ffn_out': n(ks[16], (DEPTH, D_FF, d), jnp.float32) * D_FF ** -0.5,
        'final_norm_g': 1.0 + 0.05 * n(ks[17], (d,), jnp.float32),
    }


def reference(x, c, w_ada, b_ada, norm_mix_g, w_in, conv_short_w, w_short_out,
              conv_conf_w, conv_conf_b, conf_ln_g, conf_ln_b, w_conf_out, w_o,
              norm_ffn_g, w_ffn_in, w_ffn_out, final_norm_g):
    for l in range(DEPTH):
        mod = jax.nn.silu(c) @ w_ada[l] + b_ada[l]
        sh1, sc1, g1, sh2, sc2, g2 = jnp.split(mod[:, None, :], N_MOD, axis=-1)

        h = rms_norm(x, norm_mix_g[l]) * (1.0 + sc1) + sh1
        proj = jnp.einsum('bsd,de->bse', h, w_in[l])
        b_s, c_s, v_s, v_c, gl_c, gate_br = jnp.split(proj, SPLITS[:-1], axis=-1)

        y_a = b_s * causal_dwconv(c_s * v_s, conv_short_w[l])
        y_a = jnp.einsum('bsc,cd->bsd', y_a, w_short_out[l])

        u = v_c * jax.nn.sigmoid(gl_c)
        u = causal_dwconv(u, conv_conf_w[l]) + conv_conf_b[l]
        u = jax.nn.silu(layer_norm(u, conf_ln_g[l], conf_ln_b[l]))
        y_b = jnp.einsum('bsc,cd->bsd', u, w_conf_out[l])

        g_a, g_b = jnp.split(jax.nn.sigmoid(gate_br), 2, axis=-1)
        mix = jnp.einsum('bsd,de->bse', g_a * y_a + g_b * y_b, w_o[l])
        x = x + g1 * mix

        h2 = rms_norm(x, norm_ffn_g[l]) * (1.0 + sc2) + sh2
        a, bgate = jnp.split(jnp.einsum('bsd,df->bsf', h2, w_ffn_in[l]), 2, axis=-1)
        x = x + g2 * jnp.einsum('bsf,fd->bsd', jax.nn.silu(a) * bgate, w_ffn_out[l])

    return rms_norm(x, final_norm_g)
```

```python
import functools

import jax
import jax.numpy as jnp
from jax import lax
from jax.experimental import pallas as pl
from jax.experimental.pallas import tpu as pltpu

LANES = 128
SUBLANES = 8
EPS = 1e-6
LN_EPS = 1e-5
N_MOD = 6

TM = 256
CONV_RB = 64
LN_RB = 32
VMEM_LIMIT_BYTES = 56 * 1024 * 1024

F32 = jnp.float32
BF16 = jnp.bfloat16


def _sigmoid(v):
    return 0.5 * jnp.tanh(0.5 * v) + 0.5


def _silu(v):
    hv = 0.5 * v
    return hv * jnp.tanh(hv) + hv


def _adaln_mod_kernel(c_ref, w_ref, b_ref, o_ref):
    cv = c_ref[...]
    o_ref[...] = jnp.dot(_silu(cv), w_ref[...], preferred_element_type=F32) + b_ref[...]


def _adaln_mod(c, w_ada, b_ada):
    bsz, d = c.shape
    n = w_ada.shape[1]
    return pl.pallas_call(
        _adaln_mod_kernel,
        grid=(n // d,),
        in_specs=[
            pl.BlockSpec((bsz, d), lambda j: (0, 0)),
            pl.BlockSpec((d, d), lambda j: (0, j)),
            pl.BlockSpec((1, d), lambda j: (0, j)),
        ],
        out_specs=pl.BlockSpec((bsz, d), lambda j: (0, j)),
        out_shape=jax.ShapeDtypeStruct((bsz, n), F32),
        name="adaln_mod",
    )(c, w_ada, b_ada.reshape(1, n))


def _modulated_rmsnorm(x, g, scale, shift):
    ms = jnp.mean(x * x, axis=-1, keepdims=True)
    return (x * lax.rsqrt(ms + EPS)) * (g * (1.0 + scale)) + shift


def _mix_kernel(x_ref, mod_ref, g_ref, win_ref, wcs_ref, wso_ref, wcc_ref, cb_ref, lng_ref, lnb_ref,
                wco_ref, wo_ref, o_ref, h_s, e_s, z_s, cv_s, gs_s, ya_s, ub_s, mg_s, *, d, conf_k, short_k):
    nlt = d // LANES
    e_halo = e_s.shape[1] - TM
    z_halo = z_s.shape[1] - TM
    s = pl.program_id(1)

    @pl.when(s == 0)
    def _():
        e_s[:, 0:e_halo, :] = jnp.zeros((nlt, e_halo, LANES), F32)
        z_s[:, 0:z_halo, :] = jnp.zeros((nlt, z_halo, LANES), F32)

    @pl.when(s != 0)
    def _():
        e_s[:, 0:e_halo, :] = e_s[:, TM:TM + e_halo, :]
        z_s[:, 0:z_halo, :] = z_s[:, TM:TM + z_halo, :]

    sh1 = mod_ref[0, 0:1, :]
    sc1 = mod_ref[0, 1:2, :]
    g1 = mod_ref[0, 2:3, :]

    h_s[...] = _modulated_rmsnorm(x_ref[0], g_ref[...], sc1, sh1).astype(BF16)

    cw = 2 * LANES
    for n in range(d // cw):
        p = jnp.dot(h_s[...], win_ref[:, 2 * cw * n:2 * cw * (n + 1)], preferred_element_type=F32)
        u = p[:, :cw] * _sigmoid(p[:, cw:])
        for k in range(2):
            lt = 2 * n + k
            e_s[lt, e_halo:e_halo + TM, :] = u[:, k * LANES:(k + 1) * LANES]
        for k in range(2):
            lt = 2 * n + k
            off = e_halo - (conf_k - 1)
            for rb in range(TM // CONV_RB):
                r0 = rb * CONV_RB
                acc = e_s[lt, pl.ds(r0 + off, CONV_RB), :] * wcc_ref[lt, 0:1, :]
                for j in range(1, conf_k):
                    acc = acc + e_s[lt, pl.ds(r0 + off + j, CONV_RB), :] * wcc_ref[lt, j:j + 1, :]
                cv_s[r0:r0 + CONV_RB, lt * LANES:(lt + 1) * LANES] = acc + cb_ref[:, lt * LANES:(lt + 1) * LANES]

    aw = 4 * LANES
    a_base = 2 * d
    for n in range(d // aw):
        p = jnp.dot(h_s[...], win_ref[:, a_base + 3 * aw * n:a_base + 3 * aw * (n + 1)],
                    preferred_element_type=F32)
        z = p[:, :aw] * p[:, aw:2 * aw]
        for k in range(4):
            lt = 4 * n + k
            z_s[lt, z_halo:z_halo + TM, :] = z[:, k * LANES:(k + 1) * LANES]
        for k in range(4):
            lt = 4 * n + k
            off = z_halo - (short_k - 1)
            for rb in range(TM // CONV_RB):
                r0 = rb * CONV_RB
                acc = z_s[lt, pl.ds(r0 + off, CONV_RB), :] * wcs_ref[lt, 0:1, :]
                for j in range(1, short_k):
                    acc = acc + z_s[lt, pl.ds(r0 + off + j, CONV_RB), :] * wcs_ref[lt, j:j + 1, :]
                ya = p[r0:r0 + CONV_RB, 2 * aw + k * LANES:2 * aw + (k + 1) * LANES] * acc
                ya_s[r0:r0 + CONV_RB, lt * LANES:(lt + 1) * LANES] = ya.astype(BF16)

    g_base = 5 * d
    for n in range(2 * d // aw):
        p = jnp.dot(h_s[...], win_ref[:, g_base + aw * n:g_base + aw * (n + 1)], preferred_element_type=F32)
        gs_s[:, aw * n:aw * (n + 1)] = _sigmoid(p)

    for rb in range(TM // LN_RB):
        r0 = rb * LN_RB
        v = cv_s[r0:r0 + LN_RB, :]
        mu = jnp.mean(v, axis=-1, keepdims=True)
        dv = v - mu
        var = jnp.mean(dv * dv, axis=-1, keepdims=True)
        y = dv * lax.rsqrt(var + LN_EPS) * lng_ref[...] + lnb_ref[...]
        ub_s[r0:r0 + LN_RB, :] = _silu(y).astype(BF16)

    for n in range(d // aw):
        cs = slice(aw * n, aw * (n + 1))
        y_a = jnp.dot(ya_s[...], wso_ref[:, cs], preferred_element_type=F32)
        y_b = jnp.dot(ub_s[...], wco_ref[:, cs], preferred_element_type=F32)
        m = gs_s[:, cs] * y_a + gs_s[:, d + aw * n:d + aw * (n + 1)] * y_b
        mg_s[:, cs] = m.astype(BF16)
    for n in range(d // aw):
        cs = slice(aw * n, aw * (n + 1))
        mix = jnp.dot(mg_s[...], wo_ref[:, cs], preferred_element_type=F32)
        o_ref[0, :, cs] = x_ref[0, :, cs] + g1[:, cs] * mix


def _resident(shape):
    return pl.BlockSpec(shape, lambda b, s: (0,) * len(shape), pipeline_mode=pl.Buffered(1))


def _mix_block(x, mod, g, win_p, wcs_t, wso, wcc_t, cb, lng, lnb, wco, wo):
    bsz, seq, d = x.shape
    nlt = d // LANES
    conf_k = wcc_t.shape[1]
    short_k = wcs_t.shape[1]
    e_halo = -(-(conf_k - 1) // SUBLANES) * SUBLANES
    z_halo = -(-(short_k - 1) // SUBLANES) * SUBLANES
    row = lambda: pl.BlockSpec((1, d), lambda b, s: (0, 0))
    return pl.pallas_call(
        functools.partial(_mix_kernel, d=d, conf_k=conf_k, short_k=short_k),
        grid=(bsz, seq // TM),
        in_specs=[
            pl.BlockSpec((1, TM, d), lambda b, s: (b, s, 0)),
            pl.BlockSpec((1, N_MOD, d), lambda b, s: (b, 0, 0)),
            row(),
            _resident(win_p.shape),
            _resident(wcs_t.shape),
            _resident(wso.shape),
            _resident(wcc_t.shape),
            row(), row(), row(),
            _resident(wco.shape),
            _resident(wo.shape),
        ],
        out_specs=pl.BlockSpec((1, TM, d), lambda b, s: (b, s, 0)),
        out_shape=jax.ShapeDtypeStruct(x.shape, F32),
        scratch_shapes=[
            pltpu.VMEM((TM, d), BF16),
            pltpu.VMEM((nlt, TM + e_halo, LANES), F32),
            pltpu.VMEM((nlt, TM + z_halo, LANES), F32),
            pltpu.VMEM((TM, d), F32),
            pltpu.VMEM((TM, 2 * d), F32),
            pltpu.VMEM((TM, d), BF16),
            pltpu.VMEM((TM, d), BF16),
            pltpu.VMEM((TM, d), BF16),
        ],
        compiler_params=pltpu.CompilerParams(
            dimension_semantics=("arbitrary", "arbitrary"),
            vmem_limit_bytes=VMEM_LIMIT_BYTES),
        name="mix_block",
    )(x, mod, g, win_p, wcs_t, wso, wcc_t, cb, lng, lnb, wco, wo)


def _ffn_kernel(x_ref, mod_ref, g_ref, wfi_ref, wfo_ref, fg_ref, o_ref, h_s, a_s, *, d, d_ff, final_norm):
    sh2 = mod_ref[0, 3:4, :]
    sc2 = mod_ref[0, 4:5, :]
    g2 = mod_ref[0, 5:6, :]
    h_s[...] = _modulated_rmsnorm(x_ref[0], g_ref[...], sc2, sh2).astype(BF16)
    cw = 2 * LANES
    for n in range(d_ff // cw):
        p = jnp.dot(h_s[...], wfi_ref[:, 2 * cw * n:2 * cw * (n + 1)], preferred_element_type=F32)
        a_s[:, cw * n:cw * (n + 1)] = (_silu(p[:, :cw]) * p[:, cw:]).astype(BF16)
    f = jnp.dot(a_s[...], wfo_ref[...], preferred_element_type=F32)
    x2 = x_ref[0] + g2 * f
    if final_norm:
        ms = jnp.mean(x2 * x2, axis=-1, keepdims=True)
        x2 = (x2 * lax.rsqrt(ms + EPS)) * fg_ref[...]
    o_ref[0] = x2


def _ffn_block(x, mod, g, wfi_p, wfo, fg, final_norm):
    bsz, seq, d = x.shape
    d_ff = wfo.shape[0]
    row = lambda: pl.BlockSpec((1, d), lambda b, s: (0, 0))
    return pl.pallas_call(
        functools.partial(_ffn_kernel, d=d, d_ff=d_ff, final_norm=final_norm),
        grid=(bsz, seq // TM),
        in_specs=[
            pl.BlockSpec((1, TM, d), lambda b, s: (b, s, 0)),
            pl.BlockSpec((1, N_MOD, d), lambda b, s: (b, 0, 0)),
            row(),
            _resident(wfi_p.shape),
            _resident(wfo.shape),
            row(),
        ],
        out_specs=pl.BlockSpec((1, TM, d), lambda b, s: (b, s, 0)),
        out_shape=jax.ShapeDtypeStruct(x.shape, F32),
        scratch_shapes=[
            pltpu.VMEM((TM, d), BF16),
            pltpu.VMEM((TM, d_ff), BF16),
        ],
        compiler_params=pltpu.CompilerParams(
            dimension_semantics=("arbitrary", "arbitrary"),
            vmem_limit_bytes=VMEM_LIMIT_BYTES),
        name="ffn_block",
    )(x, mod, g, wfi_p, wfo, fg)


def _chunk_interleave(parts, width):
    n_chunks = parts[0].shape[1] // width
    cols = []
    for n in range(n_chunks):
        for p in parts:
            cols.append(p[:, n * width:(n + 1) * width])
    return jnp.concatenate(cols, axis=1)


def _lane_tiled(w):
    k, ch = w.shape
    return w.reshape(k, ch // LANES, LANES).transpose(1, 0, 2)


def kernel(x, c, w_ada, b_ada, norm_mix_g, w_in, conv_short_w, w_short_out, conv_conf_w, conv_conf_b, conf_ln_g, conf_ln_b, w_conf_out, w_o, norm_ffn_g, w_ffn_in, w_ffn_out, final_norm_g):
    depth = w_ada.shape[0]
    d = x.shape[-1]
    d_ff = w_ffn_out.shape[1]
    assert x.shape[1] % TM == 0 and d % (4 * LANES) == 0 and d_ff % (2 * LANES) == 0
    fg = final_norm_g.reshape(1, d)
    for l in range(depth):
        mod = _adaln_mod(c, w_ada[l], b_ada[l]).reshape(c.shape[0], N_MOD, d)
        wi = w_in[l].astype(BF16)
        b_s, c_s, v_s, v_c, gl_c = (wi[:, i * d:(i + 1) * d] for i in range(5))
        win_p = jnp.concatenate([
            _chunk_interleave([v_c, gl_c], 2 * LANES),
            _chunk_interleave([c_s, v_s, b_s], 4 * LANES),
            wi[:, 5 * d:7 * d],
        ], axis=1)
        x = _mix_block(
            x, mod, norm_mix_g[l].reshape(1, d), win_p, _lane_tiled(conv_short_w[l]),
            w_short_out[l].astype(BF16), _lane_tiled(conv_conf_w[l]), conv_conf_b[l].reshape(1, d),
            conf_ln_g[l].reshape(1, d), conf_ln_b[l].reshape(1, d), w_conf_out[l].astype(BF16),
            w_o[l].astype(BF16))
        wf = w_ffn_in[l].astype(BF16)
        wfi_p = _chunk_interleave([wf[:, :d_ff], wf[:, d_ff:]], 2 * LANES)
        x = _ffn_block(x, mod, norm_ffn_g[l].reshape(1, d), wfi_p, w_ffn_out[l].astype(BF16), fg,
                       final_norm=(l == depth - 1))
    return x
```

```python
import functools

import jax
import jax.numpy as jnp
from jax import lax
from jax.experimental import pallas as pl
from jax.experimental.pallas import tpu as pltpu

LANES = 128
SUBLANES = 8
EPS = 1e-6
LN_EPS = 1e-5
N_MOD = 6

TM = 512
CONV_RB = 64
LN_RB = 32
VMEM_LIMIT_BYTES = 56 * 1024 * 1024

F32 = jnp.float32
BF16 = jnp.bfloat16


def _sigmoid(v):
    return 0.5 * jnp.tanh(0.5 * v) + 0.5


def _silu(v):
    hv = 0.5 * v
    return hv * jnp.tanh(hv) + hv


def _adaln_mod_kernel(c_ref, w_ref, b_ref, o_ref):
    cv = c_ref[...]
    o_ref[...] = jnp.dot(_silu(cv), w_ref[...], preferred_element_type=F32) + b_ref[...]


def _adaln_mod(c, w_ada, b_ada):
    bsz, d = c.shape
    n = w_ada.shape[1]
    return pl.pallas_call(
        _adaln_mod_kernel,
        grid=(n // d,),
        in_specs=[
            pl.BlockSpec((bsz, d), lambda j: (0, 0)),
            pl.BlockSpec((d, d), lambda j: (0, j)),
            pl.BlockSpec((1, d), lambda j: (0, j)),
        ],
        out_specs=pl.BlockSpec((bsz, d), lambda j: (0, j)),
        out_shape=jax.ShapeDtypeStruct((bsz, n), F32),
        name="adaln_mod",
    )(c, w_ada, b_ada.reshape(1, n))


def _modulated_rmsnorm(x, g, scale, shift):
    ms = jnp.mean(x * x, axis=-1, keepdims=True)
    return (x * lax.rsqrt(ms + EPS)) * (g * (1.0 + scale)) + shift


def _zero_after(v):
    b = lax.bitcast_convert_type(v, jnp.uint32)
    return lax.shift_right_logical(lax.shift_right_logical(b, jnp.uint32(16)), jnp.uint32(16))


def _after(v, zero_bits):
    zb = jnp.concatenate([zero_bits] * (v.shape[0] // SUBLANES), axis=0)
    zb = jnp.concatenate([zb] * (v.shape[1] // LANES), axis=1)
    return lax.bitcast_convert_type(lax.bitcast_convert_type(v, jnp.uint32) | zb, F32)


def _dwconv_rows(src_ref, taps_ref, slab, lt, row0, rows, n_taps, after=()):
    def tap(j):
        w = jnp.broadcast_to(taps_ref[lt, j:j + 1, :], (SUBLANES, LANES))
        for z in after:
            w = _after(w, z)
        return jnp.concatenate([w] * (rows // SUBLANES), axis=0)
    acc = src_ref[slab, pl.ds(row0, rows), :] * tap(0)
    for j in range(1, n_taps):
        acc = acc + src_ref[slab, pl.ds(row0 + j, rows), :] * tap(j)
    return acc


_GATE_CHUNK_TIES = (5, 11, 14, 20)
_SHORT_CHUNK_TIE = 27


def _mix_kernel(x_ref, mod_ref, g_ref, win_ref, wcs_ref, wso_ref, wcc_ref, cb_ref, lng_ref, lnb_ref,
                wco_ref, wo_ref, o_ref, h_s, b_s, cv_s, gs_s, ya_s, ub_s, mg_s, *conv_in, d, conf_k, short_k):
    cw = 2 * LANES
    aw = 4 * LANES
    n_b = d // cw
    e_bufs = conv_in[:n_b]
    z_bufs = conv_in[n_b:]
    e_halo = e_bufs[0].shape[1] - TM
    z_halo = z_bufs[0].shape[1] - TM
    s = pl.program_id(1)

    @pl.when(s == 0)
    def _():
        for buf, halo in [(e, e_halo) for e in e_bufs] + [(z, z_halo) for z in z_bufs]:
            buf[:, 0:halo, :] = jnp.zeros((buf.shape[0], halo, LANES), F32)

    @pl.when(s != 0)
    def _():
        for buf, halo in [(e, e_halo) for e in e_bufs] + [(z, z_halo) for z in z_bufs]:
            buf[:, 0:halo, :] = buf[:, TM:TM + halo, :]

    sh1 = mod_ref[0, 0:1, :]
    sc1 = mod_ref[0, 1:2, :]
    g1 = mod_ref[0, 2:3, :]

    h_s[...] = _modulated_rmsnorm(x_ref[0], g_ref[...], sc1, sh1).astype(BF16)

    a_base = 2 * d
    g_base = 5 * d
    rbs = TM // CONV_RB
    n_blocks = n_b * 2 * rbs
    done = {}

    def dot_b(n):
        p = jnp.dot(h_s[...], win_ref[:, 2 * cw * n:2 * cw * (n + 1)], preferred_element_type=F32)
        u = p[:, :cw] * _sigmoid(p[:, cw:])
        for k in range(2):
            e_bufs[n][k, e_halo:e_halo + TM, :] = u[:, k * LANES:(k + 1) * LANES]

    def dot_a(n):
        p = jnp.dot(h_s[...], win_ref[:, a_base + 3 * aw * n:a_base + 3 * aw * (n + 1)],
                    preferred_element_type=F32)
        z = p[:, :aw] * p[:, aw:2 * aw]
        for k in range(4):
            z_bufs[n][k, z_halo:z_halo + TM, :] = z[:, k * LANES:(k + 1) * LANES]
        b_s[:, aw * n:aw * (n + 1)] = p[:, 2 * aw:]
        done[('a', n)] = _zero_after(p[0:SUBLANES, 0:LANES])

    def dot_g(n):
        p = jnp.dot(h_s[...], win_ref[:, g_base + aw * n:g_base + aw * (n + 1)], preferred_element_type=F32)
        gs_s[:, aw * n:aw * (n + 1)] = _sigmoid(p)
        done[('g', n)] = _zero_after(p[0:SUBLANES, aw - LANES:aw])

    def conv_block(i, tied_chunk):
        n, r = divmod(i, 2 * rbs)
        k, rb = divmod(r, rbs)
        r0 = rb * CONV_RB
        cs = slice((2 * n + k) * LANES, (2 * n + k + 1) * LANES)
        after = [done[m] for m in (('blk', i - 1), tied_chunk) if m in done]
        acc = _dwconv_rows(e_bufs[n], wcc_ref, k, 2 * n + k, r0 + e_halo - (conf_k - 1), CONV_RB, conf_k,
                           after=after)
        cv_s[r0:r0 + CONV_RB, cs] = acc + cb_ref[:, cs]
        done[('blk', i)] = _zero_after(acc[0:SUBLANES, :])

    def conv_a(n):
        for k in range(4):
            cs = slice((4 * n + k) * LANES, (4 * n + k + 1) * LANES)
            for rb in range(rbs):
                r0 = rb * CONV_RB
                acc = _dwconv_rows(z_bufs[n], wcs_ref, k, 4 * n + k, r0 + z_halo - (short_k - 1), CONV_RB,
                                   short_k)
                ya_s[r0:r0 + CONV_RB, cs] = (b_s[r0:r0 + CONV_RB, cs] * acc).astype(BF16)

    dot_b(0); dot_g(0); dot_b(1); dot_g(1); dot_g(2); dot_b(2); dot_g(3); dot_b(3); dot_a(0); dot_a(1)
    tie_at = {(f * n_blocks) // 32: ('g', n) for n, f in enumerate(_GATE_CHUNK_TIES)}
    tie_at[(_SHORT_CHUNK_TIE * n_blocks) // 32] = ('a', 0)
    for i in range(n_blocks):
        conv_block(i, tie_at.get(i))
    conv_a(0); conv_a(1)

    for rb in range(TM // LN_RB):
        r0 = rb * LN_RB
        v = _after(cv_s[r0:r0 + LN_RB, :], done[('a', 1)])
        mu = jnp.mean(v, axis=-1, keepdims=True)
        dv = v - mu
        var = jnp.mean(dv * dv, axis=-1, keepdims=True)
        y = dv * lax.rsqrt(var + LN_EPS) * lng_ref[...] + lnb_ref[...]
        ub_s[r0:r0 + LN_RB, :] = _silu(y).astype(BF16)

    for n in range(d // aw):
        cs = slice(aw * n, aw * (n + 1))
        y_a = jnp.dot(ya_s[...], wso_ref[:, cs], preferred_element_type=F32)
        y_b = jnp.dot(ub_s[...], wco_ref[:, cs], preferred_element_type=F32)
        m = gs_s[:, cs] * y_a + gs_s[:, d + aw * n:d + aw * (n + 1)] * y_b
        mg_s[:, cs] = m.astype(BF16)
    for n in range(d // aw):
        cs = slice(aw * n, aw * (n + 1))
        mix = jnp.dot(mg_s[...], wo_ref[:, cs], preferred_element_type=F32)
        o_ref[0, :, cs] = x_ref[0, :, cs] + g1[:, cs] * mix


def _resident(shape):
    return pl.BlockSpec(shape, lambda b, s: (0,) * len(shape), pipeline_mode=pl.Buffered(1))


def _mix_block(x, mod, g, win_p, wcs_t, wso, wcc_t, cb, lng, lnb, wco, wo):
    bsz, seq, d = x.shape
    nlt = d // LANES
    conf_k = wcc_t.shape[1]
    short_k = wcs_t.shape[1]
    e_halo = -(-(conf_k - 1) // SUBLANES) * SUBLANES
    z_halo = -(-(short_k - 1) // SUBLANES) * SUBLANES
    row = lambda: pl.BlockSpec((1, d), lambda b, s: (0, 0))
    return pl.pallas_call(
        functools.partial(_mix_kernel, d=d, conf_k=conf_k, short_k=short_k),
        grid=(bsz, seq // TM),
        in_specs=[
            pl.BlockSpec((1, TM, d), lambda b, s: (b, s, 0)),
            pl.BlockSpec((1, N_MOD, d), lambda b, s: (b, 0, 0)),
            row(),
            _resident(win_p.shape),
            _resident(wcs_t.shape),
            _resident(wso.shape),
            _resident(wcc_t.shape),
            row(), row(), row(),
            _resident(wco.shape),
            _resident(wo.shape),
        ],
        out_specs=pl.BlockSpec((1, TM, d), lambda b, s: (b, s, 0)),
        out_shape=jax.ShapeDtypeStruct(x.shape, F32),
        scratch_shapes=[
            pltpu.VMEM((TM, d), BF16),
            pltpu.VMEM((TM, d), F32),
            pltpu.VMEM((TM, d), F32),
            pltpu.VMEM((TM, 2 * d), F32),
            pltpu.VMEM((TM, d), BF16),
            pltpu.VMEM((TM, d), BF16),
            pltpu.VMEM((TM, d), BF16),
        ] + [pltpu.VMEM((2, TM + e_halo, LANES), F32)] * (nlt // 2)
          + [pltpu.VMEM((4, TM + z_halo, LANES), F32)] * (nlt // 4),
        compiler_params=pltpu.CompilerParams(
            dimension_semantics=("arbitrary", "arbitrary"),
            vmem_limit_bytes=VMEM_LIMIT_BYTES),
        name="mix_block",
    )(x, mod, g, win_p, wcs_t, wso, wcc_t, cb, lng, lnb, wco, wo)


def _ffn_kernel(x_ref, mod_ref, g_ref, wfi_ref, wfo_ref, fg_ref, o_ref, h_s, a_s, *, d, d_ff, final_norm):
    sh2 = mod_ref[0, 3:4, :]
    sc2 = mod_ref[0, 4:5, :]
    g2 = mod_ref[0, 5:6, :]
    h_s[...] = _modulated_rmsnorm(x_ref[0], g_ref[...], sc2, sh2).astype(BF16)
    cw = 2 * LANES
    for n in range(d_ff // cw):
        p = jnp.dot(h_s[...], wfi_ref[:, 2 * cw * n:2 * cw * (n + 1)], preferred_element_type=F32)
        a_s[:, cw * n:cw * (n + 1)] = (_silu(p[:, :cw]) * p[:, cw:]).astype(BF16)
    f = jnp.dot(a_s[...], wfo_ref[:, 0:d], preferred_element_type=F32)
    x2 = x_ref[0] + g2 * f
    if final_norm:
        ms = jnp.mean(x2 * x2, axis=-1, keepdims=True)
        x2 = (x2 * lax.rsqrt(ms + EPS)) * fg_ref[...]
    o_ref[0] = x2


def _ffn_block(x, mod, g, wfi_p, wfo, fg, final_norm):
    bsz, seq, d = x.shape
    d_ff = wfo.shape[0]
    row = lambda: pl.BlockSpec((1, d), lambda b, s: (0, 0))
    return pl.pallas_call(
        functools.partial(_ffn_kernel, d=d, d_ff=d_ff, final_norm=final_norm),
        grid=(bsz, seq // TM),
        in_specs=[
            pl.BlockSpec((1, TM, d), lambda b, s: (b, s, 0)),
            pl.BlockSpec((1, N_MOD, d), lambda b, s: (b, 0, 0)),
            row(),
            _resident(wfi_p.shape),
            _resident(wfo.shape),
            row(),
        ],
        out_specs=pl.BlockSpec((1, TM, d), lambda b, s: (b, s, 0)),
        out_shape=jax.ShapeDtypeStruct(x.shape, F32),
        scratch_shapes=[
            pltpu.VMEM((TM, d), BF16),
            pltpu.VMEM((TM, d_ff), BF16),
        ],
        compiler_params=pltpu.CompilerParams(
            dimension_semantics=("arbitrary", "arbitrary"),
            vmem_limit_bytes=VMEM_LIMIT_BYTES),
        name="ffn_block",
    )(x, mod, g, wfi_p, wfo, fg)


def _mxu_weight(w):
    w = w.astype(BF16)
    if (w.shape[1] // LANES) % SUBLANES == 0:
        w = jnp.pad(w, ((0, 0), (0, LANES)))
    return w


def _chunk_interleave(parts, width):
    n_chunks = parts[0].shape[1] // width
    cols = []
    for n in range(n_chunks):
        for p in parts:
            cols.append(p[:, n * width:(n + 1) * width])
    return jnp.concatenate(cols, axis=1)


def _lane_tiled(w):
    k, ch = w.shape
    return w.reshape(k, ch // LANES, LANES).transpose(1, 0, 2)


def kernel(x, c, w_ada, b_ada, norm_mix_g, w_in, conv_short_w, w_short_out, conv_conf_w, conv_conf_b, conf_ln_g, conf_ln_b, w_conf_out, w_o, norm_ffn_g, w_ffn_in, w_ffn_out, final_norm_g):
    depth = w_ada.shape[0]
    d = x.shape[-1]
    d_ff = w_ffn_out.shape[1]
    assert x.shape[1] % TM == 0 and d == 8 * LANES and d_ff % (2 * LANES) == 0
    fg = final_norm_g.reshape(1, d)
    for l in range(depth):
        mod = _adaln_mod(c, w_ada[l], b_ada[l]).reshape(c.shape[0], N_MOD, d)
        wi = w_in[l].astype(BF16)
        b_s, c_s, v_s, v_c, gl_c = (wi[:, i * d:(i + 1) * d] for i in range(5))
        win_p = jnp.concatenate([
            _chunk_interleave([v_c, gl_c], 2 * LANES),
            _chunk_interleave([c_s, v_s, b_s], 4 * LANES),
            wi[:, 5 * d:7 * d],
        ], axis=1)
        x = _mix_block(
            x, mod, norm_mix_g[l].reshape(1, d), _mxu_weight(win_p), _lane_tiled(conv_short_w[l]),
            _mxu_weight(w_short_out[l]), _lane_tiled(conv_conf_w[l]), conv_conf_b[l].reshape(1, d),
            conf_ln_g[l].reshape(1, d), conf_ln_b[l].reshape(1, d), _mxu_weight(w_conf_out[l]),
            _mxu_weight(w_o[l]))
        wf = w_ffn_in[l].astype(BF16)
        wfi_p = _chunk_interleave([wf[:, :d_ff], wf[:, d_ff:]], 2 * LANES)
        x = _ffn_block(x, mod, norm_ffn_g[l].reshape(1, d), _mxu_weight(wfi_p), _mxu_weight(w_ffn_out[l]),
                       fg, final_norm=(l == depth - 1))
    return x
```

```python
import functools

import jax
import jax.numpy as jnp
from jax import lax
from jax.experimental import pallas as pl
from jax.experimental.pallas import tpu as pltpu

LANES = 128
SUBLANES = 8
EPS = 1e-6
LN_EPS = 1e-5
N_MOD = 6

TM = 512
CONV_RB = 128
LN_RB = 32
VMEM_LIMIT_BYTES = 56 * 1024 * 1024

F32 = jnp.float32
BF16 = jnp.bfloat16


def _sigmoid(v):
    return 0.5 * jnp.tanh(0.5 * v) + 0.5


def _silu(v):
    hv = 0.5 * v
    return hv * jnp.tanh(hv) + hv


def _adaln_mod_kernel(c_ref, w_ref, b_ref, o_ref):
    cv = c_ref[...]
    o_ref[...] = jnp.dot(_silu(cv), w_ref[...], preferred_element_type=F32) + b_ref[...]


def _adaln_mod(c, w_ada, b_ada):
    bsz, d = c.shape
    n = w_ada.shape[1]
    return pl.pallas_call(
        _adaln_mod_kernel,
        grid=(n // d,),
        in_specs=[
            pl.BlockSpec((bsz, d), lambda j: (0, 0)),
            pl.BlockSpec((d, d), lambda j: (0, j)),
            pl.BlockSpec((1, d), lambda j: (0, j)),
        ],
        out_specs=pl.BlockSpec((bsz, d), lambda j: (0, j)),
        out_shape=jax.ShapeDtypeStruct((bsz, n), F32),
        name="adaln_mod",
    )(c, w_ada, b_ada.reshape(1, n))


def _modulated_rmsnorm(x, g, scale, shift):
    ms = jnp.mean(x * x, axis=-1, keepdims=True)
    return (x * lax.rsqrt(ms + EPS)) * (g * (1.0 + scale)) + shift


def _zero_after(v):
    b = lax.bitcast_convert_type(v, jnp.uint32)
    return lax.shift_right_logical(lax.shift_right_logical(b, jnp.uint32(16)), jnp.uint32(16))


def _after(v, zero_bits):
    zb = jnp.concatenate([zero_bits] * (v.shape[0] // SUBLANES), axis=0)
    zb = jnp.concatenate([zb] * (v.shape[1] // LANES), axis=1)
    return lax.bitcast_convert_type(lax.bitcast_convert_type(v, jnp.uint32) | zb, F32)


def _dwconv_rows(src_ref, taps_ref, slab, lt, row0, rows, n_taps, after=()):
    def tap(j):
        w = jnp.broadcast_to(taps_ref[lt, j:j + 1, :], (SUBLANES, LANES))
        for z in after:
            w = _after(w, z)
        return jnp.concatenate([w] * (rows // SUBLANES), axis=0)
    acc = src_ref[slab, pl.ds(row0, rows), :] * tap(0)
    for j in range(1, n_taps):
        acc = acc + src_ref[slab, pl.ds(row0 + j, rows), :] * tap(j)
    return acc


_GATE_CHUNK_TIES = (5, 11, 14, 20)
_SHORT_CHUNK_TIE = 27


def _mix_kernel(x_ref, mod_ref, g_ref, win_ref, wcs_ref, wso_ref, wcc_ref, cb_ref, lng_ref, lnb_ref,
                wco_ref, wo_ref, o_ref, h_s, b_s, cv_s, gs_s, ya_s, ub_s, mg_s, *conv_in, d, conf_k, short_k):
    cw = 2 * LANES
    aw = 4 * LANES
    n_b = d // cw
    e_bufs = conv_in[:n_b]
    z_bufs = conv_in[n_b:]
    e_halo = e_bufs[0].shape[1] - TM
    z_halo = z_bufs[0].shape[1] - TM
    s = pl.program_id(1)

    @pl.when(s == 0)
    def _():
        for buf, halo in [(e, e_halo) for e in e_bufs] + [(z, z_halo) for z in z_bufs]:
            buf[:, 0:halo, :] = jnp.zeros((buf.shape[0], halo, LANES), F32)

    @pl.when(s != 0)
    def _():
        for buf, halo in [(e, e_halo) for e in e_bufs] + [(z, z_halo) for z in z_bufs]:
            buf[:, 0:halo, :] = buf[:, TM:TM + halo, :]

    sh1 = mod_ref[0, 0:1, :]
    sc1 = mod_ref[0, 1:2, :]
    g1 = mod_ref[0, 2:3, :]

    hm = TM // 2
    for r0, r1 in ((0, hm), (hm, TM)):
        h_s[r0:r1, :] = _modulated_rmsnorm(x_ref[0, r0:r1, :], g_ref[...], sc1, sh1).astype(BF16)

    a_base = 2 * d
    g_base = 5 * d
    rbs = TM // CONV_RB
    n_blocks = n_b * 2 * rbs
    done = {}

    def dot_b(n):
        for r0, r1 in ((0, hm), (hm, TM)) if n == 0 else ((0, TM),):
            p = jnp.dot(h_s[r0:r1, :], win_ref[:, 2 * cw * n:2 * cw * (n + 1)], preferred_element_type=F32)
            u = p[:, :cw] * _sigmoid(p[:, cw:])
            for k in range(2):
                e_bufs[n][k, e_halo + r0:e_halo + r1, :] = u[:, k * LANES:(k + 1) * LANES]

    def dot_a(n):
        p = jnp.dot(h_s[...], win_ref[:, a_base + 3 * aw * n:a_base + 3 * aw * (n + 1)],
                    preferred_element_type=F32)
        z = p[:, :aw] * p[:, aw:2 * aw]
        for k in range(4):
            z_bufs[n][k, z_halo:z_halo + TM, :] = z[:, k * LANES:(k + 1) * LANES]
        b_s[:, aw * n:aw * (n + 1)] = p[:, 2 * aw:]
        done[('a', n)] = _zero_after(p[0:SUBLANES, 0:LANES])

    def dot_g(n):
        p = jnp.dot(h_s[...], win_ref[:, g_base + aw * n:g_base + aw * (n + 1)], preferred_element_type=F32)
        gs_s[:, aw * n:aw * (n + 1)] = _sigmoid(p)
        done[('g', n)] = _zero_after(p[0:SUBLANES, aw - LANES:aw])

    def conv_block(i, tied_chunk):
        n, r = divmod(i, 2 * rbs)
        k, rb = divmod(r, rbs)
        r0 = rb * CONV_RB
        cs = slice((2 * n + k) * LANES, (2 * n + k + 1) * LANES)
        after = [done[m] for m in (('blk', i - 1), tied_chunk) if m in done]
        acc = _dwconv_rows(e_bufs[n], wcc_ref, k, 2 * n + k, r0 + e_halo - (conf_k - 1), CONV_RB, conf_k,
                           after=after)
        cv_s[r0:r0 + CONV_RB, cs] = acc + cb_ref[:, cs]
        done[('blk', i)] = _zero_after(acc[0:SUBLANES, :])

    def conv_a(n):
        for k in range(4):
            cs = slice((4 * n + k) * LANES, (4 * n + k + 1) * LANES)
            for rb in range(rbs):
                r0 = rb * CONV_RB
                acc = _dwconv_rows(z_bufs[n], wcs_ref, k, 4 * n + k, r0 + z_halo - (short_k - 1), CONV_RB,
                                   short_k)
                ya_s[r0:r0 + CONV_RB, cs] = (b_s[r0:r0 + CONV_RB, cs] * acc).astype(BF16)

    dot_b(0); dot_g(0); dot_b(1); dot_g(1); dot_g(2); dot_b(2); dot_g(3); dot_b(3); dot_a(0); dot_a(1)
    tie_at = {(f * n_blocks) // 32: ('g', n) for n, f in enumerate(_GATE_CHUNK_TIES)}
    tie_at[(_SHORT_CHUNK_TIE * n_blocks) // 32] = ('a', 0)
    for i in range(n_blocks):
        conv_block(i, tie_at.get(i))
    conv_a(0); conv_a(1)

    for rb in range(TM // LN_RB):
        r0 = rb * LN_RB
        v = _after(cv_s[r0:r0 + LN_RB, :], done[('a', 1)])
        mu = jnp.mean(v, axis=-1, keepdims=True)
        dv = v - mu
        var = jnp.mean(dv * dv, axis=-1, keepdims=True)
        y = dv * lax.rsqrt(var + LN_EPS) * lng_ref[...] + lnb_ref[...]
        ub_s[r0:r0 + LN_RB, :] = _silu(y).astype(BF16)

    for n in range(d // aw):
        cs = slice(aw * n, aw * (n + 1))
        y_a = jnp.dot(ya_s[...], wso_ref[:, cs], preferred_element_type=F32)
        y_b = jnp.dot(ub_s[...], wco_ref[:, cs], preferred_element_type=F32)
        m = gs_s[:, cs] * y_a + gs_s[:, d + aw * n:d + aw * (n + 1)] * y_b
        mg_s[:, cs] = m.astype(BF16)
    for n in range(d // aw):
        cs = slice(aw * n, aw * (n + 1))
        mix = jnp.dot(mg_s[...], wo_ref[:, cs], preferred_element_type=F32)
        o_ref[0, :, cs] = x_ref[0, :, cs] + g1[:, cs] * mix


def _resident(shape):
    return pl.BlockSpec(shape, lambda b, s: (0,) * len(shape), pipeline_mode=pl.Buffered(1))


def _mix_block(x, mod, g, win_p, wcs_t, wso, wcc_t, cb, lng, lnb, wco, wo):
    bsz, seq, d = x.shape
    nlt = d // LANES
    conf_k = wcc_t.shape[1]
    short_k = wcs_t.shape[1]
    e_halo = -(-(conf_k - 1) // SUBLANES) * SUBLANES
    z_halo = -(-(short_k - 1) // SUBLANES) * SUBLANES
    row = lambda: pl.BlockSpec((1, d), lambda b, s: (0, 0))
    return pl.pallas_call(
        functools.partial(_mix_kernel, d=d, conf_k=conf_k, short_k=short_k),
        grid=(bsz, seq // TM),
        in_specs=[
            pl.BlockSpec((1, TM, d), lambda b, s: (b, s, 0)),
            pl.BlockSpec((1, N_MOD, d), lambda b, s: (b, 0, 0)),
            row(),
            _resident(win_p.shape),
            _resident(wcs_t.shape),
            _resident(wso.shape),
            _resident(wcc_t.shape),
            row(), row(), row(),
            _resident(wco.shape),
            _resident(wo.shape),
        ],
        out_specs=pl.BlockSpec((1, TM, d), lambda b, s: (b, s, 0)),
        out_shape=jax.ShapeDtypeStruct(x.shape, F32),
        scratch_shapes=[
            pltpu.VMEM((TM, d), BF16),
            pltpu.VMEM((TM, d), F32),
            pltpu.VMEM((TM, d), F32),
            pltpu.VMEM((TM, 2 * d), F32),
            pltpu.VMEM((TM, d), BF16),
            pltpu.VMEM((TM, d), BF16),
            pltpu.VMEM((TM, d), BF16),
        ] + [pltpu.VMEM((2, TM + e_halo, LANES), F32)] * (nlt // 2)
          + [pltpu.VMEM((4, TM + z_halo, LANES), F32)] * (nlt // 4),
        compiler_params=pltpu.CompilerParams(
            dimension_semantics=("arbitrary", "arbitrary"),
            vmem_limit_bytes=VMEM_LIMIT_BYTES),
        name="mix_block",
    )(x, mod, g, win_p, wcs_t, wso, wcc_t, cb, lng, lnb, wco, wo)


def _ffn_kernel(x_ref, mod_ref, g_ref, wfi_ref, wfo_ref, fg_ref, o_ref, h_s, a_s, *, d, d_ff, final_norm):
    sh2 = mod_ref[0, 3:4, :]
    sc2 = mod_ref[0, 4:5, :]
    g2 = mod_ref[0, 5:6, :]
    halves = ((0, TM // 2), (TM // 2, TM))
    for r0, r1 in halves:
        h_s[r0:r1, :] = _modulated_rmsnorm(x_ref[0, r0:r1, :], g_ref[...], sc2, sh2).astype(BF16)
    cw = 2 * LANES
    for n in range(d_ff // cw):
        for r0, r1 in halves if n == 0 else ((0, TM),):
            p = jnp.dot(h_s[r0:r1, :], wfi_ref[:, 2 * cw * n:2 * cw * (n + 1)], preferred_element_type=F32)
            a_s[r0:r1, cw * n:cw * (n + 1)] = (_silu(p[:, :cw]) * p[:, cw:]).astype(BF16)
    for r0, r1 in halves:
        f = jnp.dot(a_s[r0:r1, :], wfo_ref[:, 0:d], preferred_element_type=F32)
        x2 = x_ref[0, r0:r1, :] + g2 * f
        if final_norm:
            ms = jnp.mean(x2 * x2, axis=-1, keepdims=True)
            x2 = (x2 * lax.rsqrt(ms + EPS)) * fg_ref[...]
        o_ref[0, r0:r1, :] = x2


def _ffn_block(x, mod, g, wfi_p, wfo, fg, final_norm):
    bsz, seq, d = x.shape
    d_ff = wfo.shape[0]
    row = lambda: pl.BlockSpec((1, d), lambda b, s: (0, 0))
    return pl.pallas_call(
        functools.partial(_ffn_kernel, d=d, d_ff=d_ff, final_norm=final_norm),
        grid=(bsz, seq // TM),
        in_specs=[
            pl.BlockSpec((1, TM, d), lambda b, s: (b, s, 0)),
            pl.BlockSpec((1, N_MOD, d), lambda b, s: (b, 0, 0)),
            row(),
            _resident(wfi_p.shape),
            _resident(wfo.shape),
            row(),
        ],
        out_specs=pl.BlockSpec((1, TM, d), lambda b, s: (b, s, 0)),
        out_shape=jax.ShapeDtypeStruct(x.shape, F32),
        scratch_shapes=[
            pltpu.VMEM((TM, d), BF16),
            pltpu.VMEM((TM, d_ff), BF16),
        ],
        compiler_params=pltpu.CompilerParams(
            dimension_semantics=("arbitrary", "arbitrary"),
            vmem_limit_bytes=VMEM_LIMIT_BYTES),
        name="ffn_block",
    )(x, mod, g, wfi_p, wfo, fg)


def _mxu_weight(*parts):
    parts = [p.astype(BF16) for p in parts]
    if (sum(p.shape[1] for p in parts) // LANES) % SUBLANES == 0:
        parts.append(jnp.zeros((parts[0].shape[0], LANES), BF16))
    return parts[0] if len(parts) == 1 else jnp.concatenate(parts, axis=1)


def _chunk_interleave(parts, width):
    k = parts[0].shape[0]
    return jnp.stack([p.reshape(k, -1, width) for p in parts], axis=2).reshape(k, -1)


def _lane_tiled(w):
    k, ch = w.shape
    return w.reshape(k, ch // LANES, LANES).transpose(1, 0, 2)


def kernel(x, c, w_ada, b_ada, norm_mix_g, w_in, conv_short_w, w_short_out, conv_conf_w, conv_conf_b, conf_ln_g, conf_ln_b, w_conf_out, w_o, norm_ffn_g, w_ffn_in, w_ffn_out, final_norm_g):
    depth = w_ada.shape[0]
    d = x.shape[-1]
    d_ff = w_ffn_out.shape[1]
    assert x.shape[1] % TM == 0 and d == 8 * LANES and d_ff % (2 * LANES) == 0
    fg = final_norm_g.reshape(1, d)
    for l in range(depth):
        mod = _adaln_mod(c, w_ada[l], b_ada[l]).reshape(c.shape[0], N_MOD, d)
        wi = w_in[l]
        b_s, c_s, v_s, v_c, gl_c = (wi[:, i * d:(i + 1) * d] for i in range(5))
        win_p = _mxu_weight(_chunk_interleave([v_c, gl_c], 2 * LANES),
                            _chunk_interleave([c_s, v_s, b_s], 4 * LANES),
                            wi[:, 5 * d:7 * d])
        x = _mix_block(
            x, mod, norm_mix_g[l].reshape(1, d), win_p, _lane_tiled(conv_short_w[l]),
            _mxu_weight(w_short_out[l]), _lane_tiled(conv_conf_w[l]), conv_conf_b[l].reshape(1, d),
            conf_ln_g[l].reshape(1, d), conf_ln_b[l].reshape(1, d), _mxu_weight(w_conf_out[l]),
            _mxu_weight(w_o[l]))
        wf = w_ffn_in[l]
        wfi_p = _chunk_interleave([wf[:, :d_ff], wf[:, d_ff:]], 2 * LANES)
        x = _ffn_block(x, mod, norm_ffn_g[l].reshape(1, d), _mxu_weight(wfi_p), _mxu_weight(w_ffn_out[l]),
                       fg, final_norm=(l == depth - 1))
    return x
```

```python
import functools

import jax
import jax.numpy as jnp
from jax import lax
from jax.experimental import pallas as pl
from jax.experimental.pallas import tpu as pltpu

LANES = 128
SUBLANES = 8
EPS = 1e-6
LN_EPS = 1e-5
N_MOD = 6

TM = 512
CONV_RB = 128
LN_RB = 32
VMEM_LIMIT_BYTES = 56 * 1024 * 1024

F32 = jnp.float32
BF16 = jnp.bfloat16


def _sigmoid(v):
    return 0.5 * jnp.tanh(0.5 * v) + 0.5


def _silu(v):
    hv = 0.5 * v
    return hv * jnp.tanh(hv) + hv


def _adaln_mod_kernel(c_ref, w_ref, b_ref, o_ref):
    cv = c_ref[...]
    o_ref[...] = jnp.dot(_silu(cv), w_ref[...], preferred_element_type=F32) + b_ref[...]


def _adaln_mod(c, w_ada, b_ada):
    bsz, d = c.shape
    n = w_ada.shape[1]
    return pl.pallas_call(
        _adaln_mod_kernel,
        grid=(n // d,),
        in_specs=[
            pl.BlockSpec((bsz, d), lambda j: (0, 0)),
            pl.BlockSpec((d, d), lambda j: (0, j)),
            pl.BlockSpec((1, d), lambda j: (0, j)),
        ],
        out_specs=pl.BlockSpec((bsz, d), lambda j: (0, j)),
        out_shape=jax.ShapeDtypeStruct((bsz, n), F32),
        name="adaln_mod",
    )(c, w_ada, b_ada.reshape(1, n))


def _modulated_rmsnorm(x, g, scale, shift):
    ms = jnp.mean(x * x, axis=-1, keepdims=True)
    return (x * lax.rsqrt(ms + EPS)) * (g * (1.0 + scale)) + shift


def _zero_after(v):
    b = lax.bitcast_convert_type(v, jnp.uint32)
    return lax.shift_right_logical(lax.shift_right_logical(b, jnp.uint32(16)), jnp.uint32(16))


def _after(v, zero_bits):
    zb = jnp.concatenate([zero_bits] * (v.shape[0] // SUBLANES), axis=0)
    zb = jnp.concatenate([zb] * (v.shape[1] // LANES), axis=1)
    return lax.bitcast_convert_type(lax.bitcast_convert_type(v, jnp.uint32) | zb, F32)


def _dwconv_rows(src_ref, taps_ref, slab, lt, row0, rows, n_taps, after=()):
    def tap(j):
        w = jnp.broadcast_to(taps_ref[lt, j:j + 1, :], (SUBLANES, LANES))
        for z in after:
            w = _after(w, z)
        return jnp.concatenate([w] * (rows // SUBLANES), axis=0)
    acc = src_ref[slab, pl.ds(row0, rows), :] * tap(0)
    for j in range(1, n_taps):
        acc = acc + src_ref[slab, pl.ds(row0 + j, rows), :] * tap(j)
    return acc


_GATE_CHUNK_TIES = (5, 11, 14, 20)
_SHORT_CHUNK_TIE = 27


def _mix_kernel(x_ref, mod_ref, g_ref, win_ref, wcs_ref, wso_ref, wcc_ref, cb_ref, lng_ref, lnb_ref,
                wco_ref, wo_ref, o_ref, h_s, b_s, cv_s, gs_s, ya_s, ub_s, mg_s, *conv_in, d, conf_k, short_k):
    cw = 2 * LANES
    aw = 4 * LANES
    n_b = d // cw
    e_bufs = conv_in[:n_b]
    z_bufs = conv_in[n_b:]
    e_halo = e_bufs[0].shape[1] - TM
    z_halo = z_bufs[0].shape[1] - TM
    s = pl.program_id(1)

    @pl.when(s == 0)
    def _():
        for buf, halo in [(e, e_halo) for e in e_bufs] + [(z, z_halo) for z in z_bufs]:
            buf[:, 0:halo, :] = jnp.zeros((buf.shape[0], halo, LANES), F32)

    @pl.when(s != 0)
    def _():
        for buf, halo in [(e, e_halo) for e in e_bufs] + [(z, z_halo) for z in z_bufs]:
            buf[:, 0:halo, :] = buf[:, TM:TM + halo, :]

    sh1 = mod_ref[0, 0:1, :]
    sc1 = mod_ref[0, 1:2, :]
    g1 = mod_ref[0, 2:3, :]

    hm = TM // 2
    for r0, r1 in ((0, hm), (hm, TM)):
        h_s[r0:r1, :] = _modulated_rmsnorm(x_ref[0, r0:r1, :], g_ref[...], sc1, sh1).astype(BF16)

    g_base = 5 * d

    def w_cols(ref, width, *starts):
        return jnp.concatenate([ref[:, c:c + width] for c in starts], axis=1)
    rbs = TM // CONV_RB
    n_blocks = n_b * 2 * rbs
    done = {}

    def dot_b(n):
        for r0, r1 in ((0, hm), (hm, TM)) if n == 0 else ((0, TM),):
            p = jnp.dot(h_s[r0:r1, :], w_cols(win_ref, cw, 3 * d + cw * n, 4 * d + cw * n),
                        preferred_element_type=F32)
            u = p[:, :cw] * _sigmoid(p[:, cw:])
            for k in range(2):
                e_bufs[n][k, e_halo + r0:e_halo + r1, :] = u[:, k * LANES:(k + 1) * LANES]

    def dot_a(n):
        p = jnp.dot(h_s[...], w_cols(win_ref, aw, d + aw * n, 2 * d + aw * n, aw * n),
                    preferred_element_type=F32)
        z = p[:, :aw] * p[:, aw:2 * aw]
        for k in range(4):
            z_bufs[n][k, z_halo:z_halo + TM, :] = z[:, k * LANES:(k + 1) * LANES]
        b_s[:, aw * n:aw * (n + 1)] = p[:, 2 * aw:]
        done[('a', n)] = _zero_after(p[0:SUBLANES, 0:LANES])

    def dot_g(n):
        p = jnp.dot(h_s[...], win_ref[:, g_base + aw * n:g_base + aw * (n + 1)], preferred_element_type=F32)
        gs_s[:, aw * n:aw * (n + 1)] = _sigmoid(p)
        done[('g', n)] = _zero_after(p[0:SUBLANES, aw - LANES:aw])

    def conv_block(i, tied_chunk):
        n, r = divmod(i, 2 * rbs)
        k, rb = divmod(r, rbs)
        r0 = rb * CONV_RB
        cs = slice((2 * n + k) * LANES, (2 * n + k + 1) * LANES)
        after = [done[m] for m in (('blk', i - 1), tied_chunk) if m in done]
        acc = _dwconv_rows(e_bufs[n], wcc_ref, k, 2 * n + k, r0 + e_halo - (conf_k - 1), CONV_RB, conf_k,
                           after=after)
        cv_s[r0:r0 + CONV_RB, cs] = acc + cb_ref[:, cs]
        done[('blk', i)] = _zero_after(acc[0:SUBLANES, :])

    def conv_a(n):
        for k in range(4):
            cs = slice((4 * n + k) * LANES, (4 * n + k + 1) * LANES)
            for rb in range(rbs):
                r0 = rb * CONV_RB
                acc = _dwconv_rows(z_bufs[n], wcs_ref, k, 4 * n + k, r0 + z_halo - (short_k - 1), CONV_RB,
                                   short_k)
                ya_s[r0:r0 + CONV_RB, cs] = (b_s[r0:r0 + CONV_RB, cs] * acc).astype(BF16)

    dot_b(0); dot_g(0); dot_b(1); dot_g(1); dot_g(2); dot_b(2); dot_g(3); dot_b(3); dot_a(0); dot_a(1)
    tie_at = {(f * n_blocks) // 32: ('g', n) for n, f in enumerate(_GATE_CHUNK_TIES)}
    tie_at[(_SHORT_CHUNK_TIE * n_blocks) // 32] = ('a', 0)
    for i in range(n_blocks):
        conv_block(i, tie_at.get(i))
    conv_a(0); conv_a(1)

    for rb in range(TM // LN_RB):
        r0 = rb * LN_RB
        v = _after(cv_s[r0:r0 + LN_RB, :], done[('a', 1)])
        mu = jnp.mean(v, axis=-1, keepdims=True)
        dv = v - mu
        var = jnp.mean(dv * dv, axis=-1, keepdims=True)
        y = dv * lax.rsqrt(var + LN_EPS) * lng_ref[...] + lnb_ref[...]
        ub_s[r0:r0 + LN_RB, :] = _silu(y).astype(BF16)

    for n in range(d // aw):
        cs = slice(aw * n, aw * (n + 1))
        y_a = jnp.dot(ya_s[...], wso_ref[:, cs], preferred_element_type=F32)
        y_b = jnp.dot(ub_s[...], wco_ref[:, cs], preferred_element_type=F32)
        m = gs_s[:, cs] * y_a + gs_s[:, d + aw * n:d + aw * (n + 1)] * y_b
        mg_s[:, cs] = m.astype(BF16)
    for n in range(d // aw):
        cs = slice(aw * n, aw * (n + 1))
        mix = jnp.dot(mg_s[...], wo_ref[:, cs], preferred_element_type=F32)
        o_ref[0, :, cs] = x_ref[0, :, cs] + g1[:, cs] * mix


def _resident(shape):
    return pl.BlockSpec(shape, lambda b, s: (0,) * len(shape), pipeline_mode=pl.Buffered(1))


def _mix_block(x, mod, g, win_p, wcs_t, wso, wcc_t, cb, lng, lnb, wco, wo):
    bsz, seq, d = x.shape
    nlt = d // LANES
    conf_k = wcc_t.shape[1]
    short_k = wcs_t.shape[1]
    e_halo = -(-(conf_k - 1) // SUBLANES) * SUBLANES
    z_halo = -(-(short_k - 1) // SUBLANES) * SUBLANES
    row = lambda: pl.BlockSpec((1, d), lambda b, s: (0, 0))
    return pl.pallas_call(
        functools.partial(_mix_kernel, d=d, conf_k=conf_k, short_k=short_k),
        grid=(bsz, seq // TM),
        in_specs=[
            pl.BlockSpec((1, TM, d), lambda b, s: (b, s, 0)),
            pl.BlockSpec((1, N_MOD, d), lambda b, s: (b, 0, 0)),
            row(),
            _resident(win_p.shape),
            _resident(wcs_t.shape),
            _resident(wso.shape),
            _resident(wcc_t.shape),
            row(), row(), row(),
            _resident(wco.shape),
            _resident(wo.shape),
        ],
        out_specs=pl.BlockSpec((1, TM, d), lambda b, s: (b, s, 0)),
        out_shape=jax.ShapeDtypeStruct(x.shape, F32),
        scratch_shapes=[
            pltpu.VMEM((TM, d), BF16),
            pltpu.VMEM((TM, d), F32),
            pltpu.VMEM((TM, d), F32),
            pltpu.VMEM((TM, 2 * d), F32),
            pltpu.VMEM((TM, d), BF16),
            pltpu.VMEM((TM, d), BF16),
            pltpu.VMEM((TM, d), BF16),
        ] + [pltpu.VMEM((2, TM + e_halo, LANES), F32)] * (nlt // 2)
          + [pltpu.VMEM((4, TM + z_halo, LANES), F32)] * (nlt // 4),
        compiler_params=pltpu.CompilerParams(
            dimension_semantics=("arbitrary", "arbitrary"),
            vmem_limit_bytes=VMEM_LIMIT_BYTES),
        name="mix_block",
    )(x, mod, g, win_p, wcs_t, wso, wcc_t, cb, lng, lnb, wco, wo)


def _ffn_kernel(x_ref, mod_ref, g_ref, wfi_ref, wfo_ref, fg_ref, o_ref, h_s, a_s, *, d, d_ff, final_norm):
    sh2 = mod_ref[0, 3:4, :]
    sc2 = mod_ref[0, 4:5, :]
    g2 = mod_ref[0, 5:6, :]
    halves = ((0, TM // 2), (TM // 2, TM))
    for r0, r1 in halves:
        h_s[r0:r1, :] = _modulated_rmsnorm(x_ref[0, r0:r1, :], g_ref[...], sc2, sh2).astype(BF16)
    cw = 2 * LANES
    for n in range(d_ff // cw):
        for r0, r1 in halves if n == 0 else ((0, TM),):
            w = jnp.concatenate([wfi_ref[:, cw * n:cw * (n + 1)], wfi_ref[:, d_ff + cw * n:d_ff + cw * (n + 1)]],
                                axis=1)
            p = jnp.dot(h_s[r0:r1, :], w, preferred_element_type=F32)
            a_s[r0:r1, cw * n:cw * (n + 1)] = (_silu(p[:, :cw]) * p[:, cw:]).astype(BF16)
    for r0, r1 in halves:
        f = jnp.dot(a_s[r0:r1, :], wfo_ref[:, 0:d], preferred_element_type=F32)
        x2 = x_ref[0, r0:r1, :] + g2 * f
        if final_norm:
            ms = jnp.mean(x2 * x2, axis=-1, keepdims=True)
            x2 = (x2 * lax.rsqrt(ms + EPS)) * fg_ref[...]
        o_ref[0, r0:r1, :] = x2


def _ffn_block(x, mod, g, wfi_p, wfo, fg, final_norm):
    bsz, seq, d = x.shape
    d_ff = wfo.shape[0]
    row = lambda: pl.BlockSpec((1, d), lambda b, s: (0, 0))
    return pl.pallas_call(
        functools.partial(_ffn_kernel, d=d, d_ff=d_ff, final_norm=final_norm),
        grid=(bsz, seq // TM),
        in_specs=[
            pl.BlockSpec((1, TM, d), lambda b, s: (b, s, 0)),
            pl.BlockSpec((1, N_MOD, d), lambda b, s: (b, 0, 0)),
            row(),
            _resident(wfi_p.shape),
            _resident(wfo.shape),
            row(),
        ],
        out_specs=pl.BlockSpec((1, TM, d), lambda b, s: (b, s, 0)),
        out_shape=jax.ShapeDtypeStruct(x.shape, F32),
        scratch_shapes=[
            pltpu.VMEM((TM, d), BF16),
            pltpu.VMEM((TM, d_ff), BF16),
        ],
        compiler_params=pltpu.CompilerParams(
            dimension_semantics=("arbitrary", "arbitrary"),
            vmem_limit_bytes=VMEM_LIMIT_BYTES),
        name="ffn_block",
    )(x, mod, g, wfi_p, wfo, fg)


def _mxu_weight(w):
    w = w.astype(BF16)
    if (w.shape[1] // LANES) % SUBLANES == 0:
        w = jnp.pad(w, ((0, 0), (0, LANES)))
    return w


def _lane_tiled(w):
    k, ch = w.shape
    return w.reshape(k, ch // LANES, LANES).transpose(1, 0, 2)


def kernel(x, c, w_ada, b_ada, norm_mix_g, w_in, conv_short_w, w_short_out, conv_conf_w, conv_conf_b, conf_ln_g, conf_ln_b, w_conf_out, w_o, norm_ffn_g, w_ffn_in, w_ffn_out, final_norm_g):
    depth = w_ada.shape[0]
    d = x.shape[-1]
    d_ff = w_ffn_out.shape[1]
    assert x.shape[1] % TM == 0 and d == 8 * LANES and d_ff % (2 * LANES) == 0
    fg = final_norm_g.reshape(1, d)
    for l in range(depth):
        mod = _adaln_mod(c, w_ada[l], b_ada[l]).reshape(c.shape[0], N_MOD, d)
        x = _mix_block(
            x, mod, norm_mix_g[l].reshape(1, d), _mxu_weight(w_in[l]), _lane_tiled(conv_short_w[l]),
            _mxu_weight(w_short_out[l]), _lane_tiled(conv_conf_w[l]), conv_conf_b[l].reshape(1, d),
            conf_ln_g[l].reshape(1, d), conf_ln_b[l].reshape(1, d), _mxu_weight(w_conf_out[l]),
            _mxu_weight(w_o[l]))
        x = _ffn_block(x, mod, norm_ffn_g[l].reshape(1, d), _mxu_weight(w_ffn_in[l]), _mxu_weight(w_ffn_out[l]),
                       fg, final_norm=(l == depth - 1))
    return x
```

```python
import functools

import jax
import jax.numpy as jnp
from jax import lax
from jax.experimental import pallas as pl
from jax.experimental.pallas import tpu as pltpu

LANES = 128
SUBLANES = 8
EPS = 1e-6
LN_EPS = 1e-5
N_MOD = 6

TM = 512
CONV_RB = 128
LN_RB = 32
VMEM_LIMIT_BYTES = 56 * 1024 * 1024

F32 = jnp.float32
BF16 = jnp.bfloat16


def _sigmoid(v):
    return 0.5 * jnp.tanh(0.5 * v) + 0.5


def _silu(v):
    hv = 0.5 * v
    return hv * jnp.tanh(hv) + hv


def _adaln_mod_kernel(c_ref, w_ref, b_ref, o_ref):
    cv = c_ref[...]
    o_ref[...] = jnp.dot(_silu(cv), w_ref[...], preferred_element_type=F32) + b_ref[...]


def _adaln_mod(c, w_ada, b_ada):
    bsz, d = c.shape
    n = w_ada.shape[1]
    return pl.pallas_call(
        _adaln_mod_kernel,
        grid=(n // d,),
        in_specs=[
            pl.BlockSpec((bsz, d), lambda j: (0, 0)),
            pl.BlockSpec((d, d), lambda j: (0, j)),
            pl.BlockSpec((1, d), lambda j: (0, j)),
        ],
        out_specs=pl.BlockSpec((bsz, d), lambda j: (0, j)),
        out_shape=jax.ShapeDtypeStruct((bsz, n), F32),
        name="adaln_mod",
    )(c, w_ada, b_ada.reshape(1, n))


def _modulated_rmsnorm(x, g, scale, shift):
    ms = jnp.mean(x * x, axis=-1, keepdims=True)
    return (x * lax.rsqrt(ms + EPS)) * (g * (1.0 + scale)) + shift


def _zero_after(v):
    b = lax.bitcast_convert_type(v, jnp.uint32)
    return lax.shift_right_logical(lax.shift_right_logical(b, jnp.uint32(16)), jnp.uint32(16))


def _after(v, zero_bits):
    zb = jnp.concatenate([zero_bits] * (v.shape[0] // SUBLANES), axis=0)
    zb = jnp.concatenate([zb] * (v.shape[1] // LANES), axis=1)
    return lax.bitcast_convert_type(lax.bitcast_convert_type(v, jnp.uint32) | zb, F32)


def _dwconv_rows(src_ref, taps_ref, slab, lt, row0, rows, n_taps):
    acc = src_ref[slab, pl.ds(row0, rows), :] * taps_ref[lt, 0:1, :]
    for j in range(1, n_taps):
        acc = acc + src_ref[slab, pl.ds(row0 + j, rows), :] * taps_ref[lt, j:j + 1, :]
    return acc


def _mix_kernel(x_ref, mod_ref, g_ref, win_ref, wcs_ref, wso_ref, wcc_ref, cb_ref, lng_ref, lnb_ref,
                wco_ref, wo_ref, o_ref, h_s, b_s, cv_s, gs_s, ya_s, ub_s, mg_s, *conv_in, d, conf_k, short_k):
    cw = 2 * LANES
    aw = 4 * LANES
    n_b = d // cw
    e_bufs = conv_in[:n_b]
    z_bufs = conv_in[n_b:]
    e_halo = e_bufs[0].shape[1] - TM
    z_halo = z_bufs[0].shape[1] - TM
    s = pl.program_id(1)

    @pl.when(s == 0)
    def _():
        for buf, halo in [(e, e_halo) for e in e_bufs] + [(z, z_halo) for z in z_bufs]:
            buf[:, 0:halo, :] = jnp.zeros((buf.shape[0], halo, LANES), F32)

    @pl.when(s != 0)
    def _():
        for buf, halo in [(e, e_halo) for e in e_bufs] + [(z, z_halo) for z in z_bufs]:
            buf[:, 0:halo, :] = buf[:, TM:TM + halo, :]

    sh1 = mod_ref[0, 0:1, :]
    sc1 = mod_ref[0, 1:2, :]
    g1 = mod_ref[0, 2:3, :]

    hm = TM // 2
    for r0, r1 in ((0, hm), (hm, TM)):
        h_s[r0:r1, :] = _modulated_rmsnorm(x_ref[0, r0:r1, :], g_ref[...], sc1, sh1).astype(BF16)

    g_base = 5 * d

    def w_cols(ref, width, *starts):
        return jnp.concatenate([ref[:, c:c + width] for c in starts], axis=1)
    rbs = TM // CONV_RB

    def dot_b(n):
        for r0, r1 in ((0, hm), (hm, TM)) if n == 0 else ((0, TM),):
            p = jnp.dot(h_s[r0:r1, :], w_cols(win_ref, cw, 3 * d + cw * n, 4 * d + cw * n),
                        preferred_element_type=F32)
            u = p[:, :cw] * _sigmoid(p[:, cw:])
            for k in range(2):
                e_bufs[n][k, e_halo + r0:e_halo + r1, :] = u[:, k * LANES:(k + 1) * LANES]

    def dot_a(n):
        p = jnp.dot(h_s[...], w_cols(win_ref, aw, d + aw * n, 2 * d + aw * n, aw * n),
                    preferred_element_type=F32)
        z = p[:, :aw] * p[:, aw:2 * aw]
        for k in range(4):
            z_bufs[n][k, z_halo:z_halo + TM, :] = z[:, k * LANES:(k + 1) * LANES]
        b_s[:, aw * n:aw * (n + 1)] = p[:, 2 * aw:]
        return _zero_after(p[0:SUBLANES, 0:LANES])

    def dot_g(n):
        p = jnp.dot(h_s[...], win_ref[:, g_base + aw * n:g_base + aw * (n + 1)], preferred_element_type=F32)
        gs_s[:, aw * n:aw * (n + 1)] = _sigmoid(p)

    def conv_b(n):
        for k in range(2):
            cs = slice((2 * n + k) * LANES, (2 * n + k + 1) * LANES)
            for rb in range(rbs):
                r0 = rb * CONV_RB
                acc = _dwconv_rows(e_bufs[n], wcc_ref, k, 2 * n + k, r0 + e_halo - (conf_k - 1), CONV_RB, conf_k)
                cv_s[r0:r0 + CONV_RB, cs] = acc + cb_ref[:, cs]

    def conv_a(n):
        for k in range(4):
            cs = slice((4 * n + k) * LANES, (4 * n + k + 1) * LANES)
            for rb in range(rbs):
                r0 = rb * CONV_RB
                acc = _dwconv_rows(z_bufs[n], wcs_ref, k, 4 * n + k, r0 + z_halo - (short_k - 1), CONV_RB,
                                   short_k)
                ya_s[r0:r0 + CONV_RB, cs] = (b_s[r0:r0 + CONV_RB, cs] * acc).astype(BF16)

    for n in range(n_b):
        dot_b(n)
        conv_b(n)
    for n in range(2 * d // aw):
        dot_g(n)
    dot_a(0)
    last_a_started = dot_a(1)
    conv_a(0); conv_a(1)

    for rb in range(TM // LN_RB):
        r0 = rb * LN_RB
        v = _after(cv_s[r0:r0 + LN_RB, :], last_a_started)
        mu = jnp.mean(v, axis=-1, keepdims=True)
        dv = v - mu
        var = jnp.mean(dv * dv, axis=-1, keepdims=True)
        y = dv * lax.rsqrt(var + LN_EPS) * lng_ref[...] + lnb_ref[...]
        ub_s[r0:r0 + LN_RB, :] = _silu(y).astype(BF16)

    for n in range(d // aw):
        cs = slice(aw * n, aw * (n + 1))
        y_a = jnp.dot(ya_s[...], wso_ref[:, cs], preferred_element_type=F32)
        y_b = jnp.dot(ub_s[...], wco_ref[:, cs], preferred_element_type=F32)
        m = gs_s[:, cs] * y_a + gs_s[:, d + aw * n:d + aw * (n + 1)] * y_b
        mg_s[:, cs] = m.astype(BF16)
    for n in range(d // aw):
        cs = slice(aw * n, aw * (n + 1))
        mix = jnp.dot(mg_s[...], wo_ref[:, cs], preferred_element_type=F32)
        o_ref[0, :, cs] = x_ref[0, :, cs] + g1[:, cs] * mix


def _resident(shape):
    return pl.BlockSpec(shape, lambda b, s: (0,) * len(shape), pipeline_mode=pl.Buffered(1))


def _mix_block(x, mod, g, win_p, wcs_t, wso, wcc_t, cb, lng, lnb, wco, wo):
    bsz, seq, d = x.shape
    nlt = d // LANES
    conf_k = wcc_t.shape[1]
    short_k = wcs_t.shape[1]
    e_halo = -(-(conf_k - 1) // SUBLANES) * SUBLANES
    z_halo = -(-(short_k - 1) // SUBLANES) * SUBLANES
    row = lambda: pl.BlockSpec((1, d), lambda b, s: (0, 0))
    return pl.pallas_call(
        functools.partial(_mix_kernel, d=d, conf_k=conf_k, short_k=short_k),
        grid=(bsz, seq // TM),
        in_specs=[
            pl.BlockSpec((1, TM, d), lambda b, s: (b, s, 0)),
            pl.BlockSpec((1, N_MOD, d), lambda b, s: (b, 0, 0)),
            row(),
            _resident(win_p.shape),
            _resident(wcs_t.shape),
            _resident(wso.shape),
            _resident(wcc_t.shape),
            row(), row(), row(),
            _resident(wco.shape),
            _resident(wo.shape),
        ],
        out_specs=pl.BlockSpec((1, TM, d), lambda b, s: (b, s, 0)),
        out_shape=jax.ShapeDtypeStruct(x.shape, F32),
        scratch_shapes=[
            pltpu.VMEM((TM, d), BF16),
            pltpu.VMEM((TM, d), F32),
            pltpu.VMEM((TM, d), F32),
            pltpu.VMEM((TM, 2 * d), F32),
            pltpu.VMEM((TM, d), BF16),
            pltpu.VMEM((TM, d), BF16),
            pltpu.VMEM((TM, d), BF16),
        ] + [pltpu.VMEM((2, TM + e_halo, LANES), F32)] * (nlt // 2)
          + [pltpu.VMEM((4, TM + z_halo, LANES), F32)] * (nlt // 4),
        compiler_params=pltpu.CompilerParams(
            dimension_semantics=("arbitrary", "arbitrary"),
            vmem_limit_bytes=VMEM_LIMIT_BYTES),
        name="mix_block",
    )(x, mod, g, win_p, wcs_t, wso, wcc_t, cb, lng, lnb, wco, wo)


def _ffn_kernel(x_ref, mod_ref, g_ref, wfi_ref, wfo_ref, fg_ref, o_ref, h_s, a_s, *, d, d_ff, final_norm):
    sh2 = mod_ref[0, 3:4, :]
    sc2 = mod_ref[0, 4:5, :]
    g2 = mod_ref[0, 5:6, :]
    halves = ((0, TM // 2), (TM // 2, TM))
    for r0, r1 in halves:
        h_s[r0:r1, :] = _modulated_rmsnorm(x_ref[0, r0:r1, :], g_ref[...], sc2, sh2).astype(BF16)
    cw = 2 * LANES
    for n in range(d_ff // cw):
        for r0, r1 in halves if n == 0 else ((0, TM),):
            w = jnp.concatenate([wfi_ref[:, cw * n:cw * (n + 1)], wfi_ref[:, d_ff + cw * n:d_ff + cw * (n + 1)]],
                                axis=1)
            p = jnp.dot(h_s[r0:r1, :], w, preferred_element_type=F32)
            a_s[r0:r1, cw * n:cw * (n + 1)] = (_silu(p[:, :cw]) * p[:, cw:]).astype(BF16)
    for r0, r1 in halves:
        f = jnp.dot(a_s[r0:r1, :], wfo_ref[:, 0:d], preferred_element_type=F32)
        x2 = x_ref[0, r0:r1, :] + g2 * f
        if final_norm:
            ms = jnp.mean(x2 * x2, axis=-1, keepdims=True)
            x2 = (x2 * lax.rsqrt(ms + EPS)) * fg_ref[...]
        o_ref[0, r0:r1, :] = x2


def _ffn_block(x, mod, g, wfi_p, wfo, fg, final_norm):
    bsz, seq, d = x.shape
    d_ff = wfo.shape[0]
    row = lambda: pl.BlockSpec((1, d), lambda b, s: (0, 0))
    return pl.pallas_call(
        functools.partial(_ffn_kernel, d=d, d_ff=d_ff, final_norm=final_norm),
        grid=(bsz, seq // TM),
        in_specs=[
            pl.BlockSpec((1, TM, d), lambda b, s: (b, s, 0)),
            pl.BlockSpec((1, N_MOD, d), lambda b, s: (b, 0, 0)),
            row(),
            _resident(wfi_p.shape),
            _resident(wfo.shape),
            row(),
        ],
        out_specs=pl.BlockSpec((1, TM, d), lambda b, s: (b, s, 0)),
        out_shape=jax.ShapeDtypeStruct(x.shape, F32),
        scratch_shapes=[
            pltpu.VMEM((TM, d), BF16),
            pltpu.VMEM((TM, d_ff), BF16),
        ],
        compiler_params=pltpu.CompilerParams(
            dimension_semantics=("arbitrary", "arbitrary"),
            vmem_limit_bytes=VMEM_LIMIT_BYTES),
        name="ffn_block",
    )(x, mod, g, wfi_p, wfo, fg)


def _mxu_weight(w):
    w = w.astype(BF16)
    if (w.shape[1] // LANES) % SUBLANES == 0:
        w = jnp.pad(w, ((0, 0), (0, LANES)))
    return w


def _lane_tiled(w):
    k, ch = w.shape
    return w.reshape(k, ch // LANES, LANES).transpose(1, 0, 2)


def kernel(x, c, w_ada, b_ada, norm_mix_g, w_in, conv_short_w, w_short_out, conv_conf_w, conv_conf_b, conf_ln_g, conf_ln_b, w_conf_out, w_o, norm_ffn_g, w_ffn_in, w_ffn_out, final_norm_g):
    depth = w_ada.shape[0]
    d = x.shape[-1]
    d_ff = w_ffn_out.shape[1]
    assert x.shape[1] % TM == 0 and d == 8 * LANES and d_ff % (2 * LANES) == 0
    fg = final_norm_g.reshape(1, d)
    for l in range(depth):
        mod = _adaln_mod(c, w_ada[l], b_ada[l]).reshape(c.shape[0], N_MOD, d)
        x = _mix_block(
            x, mod, norm_mix_g[l].reshape(1, d), _mxu_weight(w_in[l]), _lane_tiled(conv_short_w[l]),
            _mxu_weight(w_short_out[l]), _lane_tiled(conv_conf_w[l]), conv_conf_b[l].reshape(1, d),
            conf_ln_g[l].reshape(1, d), conf_ln_b[l].reshape(1, d), _mxu_weight(w_conf_out[l]),
            _mxu_weight(w_o[l]))
        x = _ffn_block(x, mod, norm_ffn_g[l].reshape(1, d), _mxu_weight(w_ffn_in[l]), _mxu_weight(w_ffn_out[l]),
                       fg, final_norm=(l == depth - 1))
    return x
```

```python
import functools

import jax
import jax.numpy as jnp
from jax import lax
from jax.experimental import pallas as pl
from jax.experimental.pallas import tpu as pltpu

LANES = 128
SUBLANES = 8
EPS = 1e-6
LN_EPS = 1e-5
N_MOD = 6

TM = 512
TM_FFN = 1024
CONV_RB = 512
LN_RB = 32
B_TILES = 2
VMEM_LIMIT_BYTES = 56 * 1024 * 1024

F32 = jnp.float32
BF16 = jnp.bfloat16


def _sigmoid(v):
    return 0.5 * jnp.tanh(0.5 * v) + 0.5


def _silu(v):
    hv = 0.5 * v
    return hv * jnp.tanh(hv) + hv


def _adaln_mod_kernel(c_ref, w_ref, b_ref, o_ref):
    cv = c_ref[...]
    o_ref[...] = jnp.dot(_silu(cv), w_ref[...], preferred_element_type=F32) + b_ref[...]


def _adaln_mod(c, w_ada, b_ada):
    bsz, d = c.shape
    n = w_ada.shape[1]
    return pl.pallas_call(
        _adaln_mod_kernel,
        grid=(n // d,),
        in_specs=[
            pl.BlockSpec((bsz, d), lambda j: (0, 0)),
            pl.BlockSpec((d, d), lambda j: (0, j)),
            pl.BlockSpec((1, d), lambda j: (0, j)),
        ],
        out_specs=pl.BlockSpec((bsz, d), lambda j: (0, j)),
        out_shape=jax.ShapeDtypeStruct((bsz, n), F32),
        name="adaln_mod",
    )(c, w_ada, b_ada.reshape(1, n))


def _modulated_rmsnorm(x, g, scale, shift):
    ms = jnp.mean(x * x, axis=-1, keepdims=True)
    return (x * lax.rsqrt(ms + EPS)) * (g * (1.0 + scale)) + shift


def _zero_after(v):
    b = lax.bitcast_convert_type(v, jnp.uint32)
    return lax.shift_right_logical(lax.shift_right_logical(b, jnp.uint32(16)), jnp.uint32(16))


def _after(v, zero_bits):
    zb = jnp.concatenate([zero_bits] * (v.shape[0] // SUBLANES), axis=0)
    zb = jnp.concatenate([zb] * (v.shape[1] // LANES), axis=1)
    return lax.bitcast_convert_type(lax.bitcast_convert_type(v, jnp.uint32) | zb, F32)


def _dwconv_rows(src_ref, taps_ref, slab, lt, row0, rows, n_taps):
    acc = src_ref[slab, pl.ds(row0, rows), :] * taps_ref[lt, 0:1, :]
    for j in range(1, n_taps):
        acc = acc + src_ref[slab, pl.ds(row0 + j, rows), :] * taps_ref[lt, j:j + 1, :]
    return acc


def _mix_kernel(x_ref, mod_ref, g_ref, win_ref, wcs_ref, wso_ref, wcc_ref, cb_ref, lng_ref, lnb_ref,
                wco_ref, wo_ref, o_ref, h_s, b_s, cv_s, gs_s, ya_s, ub_s, mg_s, *conv_in, d, conf_k, short_k):
    cw = B_TILES * LANES
    aw = 4 * LANES
    n_b = d // cw
    e_bufs = conv_in[:n_b]
    z_bufs = conv_in[n_b:]
    e_halo = e_bufs[0].shape[1] - TM
    z_halo = z_bufs[0].shape[1] - TM
    s = pl.program_id(1)

    @pl.when(s == 0)
    def _():
        for buf, halo in [(e, e_halo) for e in e_bufs] + [(z, z_halo) for z in z_bufs]:
            buf[:, 0:halo, :] = jnp.zeros((buf.shape[0], halo, LANES), F32)

    @pl.when(s != 0)
    def _():
        for buf, halo in [(e, e_halo) for e in e_bufs] + [(z, z_halo) for z in z_bufs]:
            buf[:, 0:halo, :] = buf[:, TM:TM + halo, :]

    sh1 = mod_ref[0, 0:1, :]
    sc1 = mod_ref[0, 1:2, :]
    g1 = mod_ref[0, 2:3, :]

    hm = TM // 2
    for r0, r1 in ((0, hm), (hm, TM)):
        h_s[r0:r1, :] = _modulated_rmsnorm(x_ref[0, r0:r1, :], g_ref[...], sc1, sh1).astype(BF16)

    g_base = 5 * d

    def w_cols(ref, width, *starts):
        return jnp.concatenate([ref[:, c:c + width] for c in starts], axis=1)
    rbs = TM // CONV_RB

    def dot_b(n):
        for r0, r1 in ((0, hm), (hm, TM)) if n == 0 else ((0, TM),):
            p = jnp.dot(h_s[r0:r1, :], w_cols(win_ref, cw, 3 * d + cw * n, 4 * d + cw * n),
                        preferred_element_type=F32)
            u = p[:, :cw] * _sigmoid(p[:, cw:])
            for k in range(B_TILES):
                e_bufs[n][k, e_halo + r0:e_halo + r1, :] = u[:, k * LANES:(k + 1) * LANES]

    def dot_a(n):
        p = jnp.dot(h_s[...], w_cols(win_ref, aw, d + aw * n, 2 * d + aw * n, aw * n),
                    preferred_element_type=F32)
        z = p[:, :aw] * p[:, aw:2 * aw]
        for k in range(4):
            z_bufs[n][k, z_halo:z_halo + TM, :] = z[:, k * LANES:(k + 1) * LANES]
        b_s[:, aw * n:aw * (n + 1)] = p[:, 2 * aw:]
        return _zero_after(p[0:SUBLANES, 0:LANES])

    def dot_g(n):
        p = jnp.dot(h_s[...], win_ref[:, g_base + aw * n:g_base + aw * (n + 1)], preferred_element_type=F32)
        gs_s[:, aw * n:aw * (n + 1)] = _sigmoid(p)

    def conv_b(n):
        for k in range(B_TILES):
            cs = slice((B_TILES * n + k) * LANES, (B_TILES * n + k + 1) * LANES)
            for rb in range(rbs):
                r0 = rb * CONV_RB
                acc = _dwconv_rows(e_bufs[n], wcc_ref, k, B_TILES * n + k, r0 + e_halo - (conf_k - 1), CONV_RB,
                                   conf_k)
                cv_s[r0:r0 + CONV_RB, cs] = acc + cb_ref[:, cs]

    def conv_a(n):
        for k in range(4):
            cs = slice((4 * n + k) * LANES, (4 * n + k + 1) * LANES)
            for rb in range(rbs):
                r0 = rb * CONV_RB
                acc = _dwconv_rows(z_bufs[n], wcs_ref, k, 4 * n + k, r0 + z_halo - (short_k - 1), CONV_RB,
                                   short_k)
                ya_s[r0:r0 + CONV_RB, cs] = (b_s[r0:r0 + CONV_RB, cs] * acc).astype(BF16)

    for n in range(n_b):
        dot_b(n)
        conv_b(n)
    for n in range(2 * d // aw):
        dot_g(n)
    dot_a(0)
    last_a_started = dot_a(1)
    conv_a(0); conv_a(1)

    for rb in range(TM // LN_RB):
        r0 = rb * LN_RB
        v = _after(cv_s[r0:r0 + LN_RB, :], last_a_started)
        mu = jnp.mean(v, axis=-1, keepdims=True)
        dv = v - mu
        var = jnp.mean(dv * dv, axis=-1, keepdims=True)
        y = dv * lax.rsqrt(var + LN_EPS) * lng_ref[...] + lnb_ref[...]
        ub_s[r0:r0 + LN_RB, :] = _silu(y).astype(BF16)

    for n in range(d // aw):
        cs = slice(aw * n, aw * (n + 1))
        y_a = jnp.dot(ya_s[...], wso_ref[:, cs], preferred_element_type=F32)
        y_b = jnp.dot(ub_s[...], wco_ref[:, cs], preferred_element_type=F32)
        m = gs_s[:, cs] * y_a + gs_s[:, d + aw * n:d + aw * (n + 1)] * y_b
        mg_s[:, cs] = m.astype(BF16)
    for n in range(d // aw):
        cs = slice(aw * n, aw * (n + 1))
        mix = jnp.dot(mg_s[...], wo_ref[:, cs], preferred_element_type=F32)
        o_ref[0, :, cs] = x_ref[0, :, cs] + g1[:, cs] * mix


def _resident(shape):
    return pl.BlockSpec(shape, lambda b, s: (0,) * len(shape), pipeline_mode=pl.Buffered(1))


def _mix_block(x, mod, g, win_p, wcs_t, wso, wcc_t, cb, lng, lnb, wco, wo):
    bsz, seq, d = x.shape
    nlt = d // LANES
    conf_k = wcc_t.shape[1]
    short_k = wcs_t.shape[1]
    e_halo = -(-(conf_k - 1) // SUBLANES) * SUBLANES
    z_halo = -(-(short_k - 1) // SUBLANES) * SUBLANES
    row = lambda: pl.BlockSpec((1, d), lambda b, s: (0, 0))
    return pl.pallas_call(
        functools.partial(_mix_kernel, d=d, conf_k=conf_k, short_k=short_k),
        grid=(bsz, seq // TM),
        in_specs=[
            pl.BlockSpec((1, TM, d), lambda b, s: (b, s, 0)),
            pl.BlockSpec((1, N_MOD, d), lambda b, s: (b, 0, 0)),
            row(),
            _resident(win_p.shape),
            _resident(wcs_t.shape),
            _resident(wso.shape),
            _resident(wcc_t.shape),
            row(), row(), row(),
            _resident(wco.shape),
            _resident(wo.shape),
        ],
        out_specs=pl.BlockSpec((1, TM, d), lambda b, s: (b, s, 0)),
        out_shape=jax.ShapeDtypeStruct(x.shape, F32),
        scratch_shapes=[
            pltpu.VMEM((TM, d), BF16),
            pltpu.VMEM((TM, d), F32),
            pltpu.VMEM((TM, d), F32),
            pltpu.VMEM((TM, 2 * d), F32),
            pltpu.VMEM((TM, d), BF16),
            pltpu.VMEM((TM, d), BF16),
            pltpu.VMEM((TM, d), BF16),
        ] + [pltpu.VMEM((B_TILES, TM + e_halo, LANES), F32)] * (nlt // B_TILES)
          + [pltpu.VMEM((4, TM + z_halo, LANES), F32)] * (nlt // 4),
        compiler_params=pltpu.CompilerParams(
            dimension_semantics=("arbitrary", "arbitrary"),
            vmem_limit_bytes=VMEM_LIMIT_BYTES),
        name="mix_block",
    )(x, mod, g, win_p, wcs_t, wso, wcc_t, cb, lng, lnb, wco, wo)


def _ffn_kernel(x_ref, mod_ref, g_ref, wfi_ref, wfo_ref, fg_ref, o_ref, h_s, a_s, *, d, d_ff, final_norm):
    sh2 = mod_ref[0, 3:4, :]
    sc2 = mod_ref[0, 4:5, :]
    g2 = mod_ref[0, 5:6, :]
    tm = x_ref.shape[1]
    halves = ((0, tm // 2), (tm // 2, tm))
    for r0, r1 in halves:
        h_s[r0:r1, :] = _modulated_rmsnorm(x_ref[0, r0:r1, :], g_ref[...], sc2, sh2).astype(BF16)
    cw = 2 * LANES
    for n in range(d_ff // cw):
        for r0, r1 in halves if n == 0 else ((0, tm),):
            w = jnp.concatenate([wfi_ref[:, cw * n:cw * (n + 1)], wfi_ref[:, d_ff + cw * n:d_ff + cw * (n + 1)]],
                                axis=1)
            p = jnp.dot(h_s[r0:r1, :], w, preferred_element_type=F32)
            a_s[r0:r1, cw * n:cw * (n + 1)] = (_silu(p[:, :cw]) * p[:, cw:]).astype(BF16)
    for r0, r1 in halves:
        f = jnp.dot(a_s[r0:r1, :], wfo_ref[:, 0:d], preferred_element_type=F32)
        x2 = x_ref[0, r0:r1, :] + g2 * f
        if final_norm:
            ms = jnp.mean(x2 * x2, axis=-1, keepdims=True)
            x2 = (x2 * lax.rsqrt(ms + EPS)) * fg_ref[...]
        o_ref[0, r0:r1, :] = x2


def _ffn_block(x, mod, g, wfi_p, wfo, fg, final_norm):
    bsz, seq, d = x.shape
    d_ff = wfo.shape[0]
    row = lambda: pl.BlockSpec((1, d), lambda b, s: (0, 0))
    return pl.pallas_call(
        functools.partial(_ffn_kernel, d=d, d_ff=d_ff, final_norm=final_norm),
        grid=(bsz, seq // TM_FFN),
        in_specs=[
            pl.BlockSpec((1, TM_FFN, d), lambda b, s: (b, s, 0)),
            pl.BlockSpec((1, N_MOD, d), lambda b, s: (b, 0, 0)),
            row(),
            _resident(wfi_p.shape),
            _resident(wfo.shape),
            row(),
        ],
        out_specs=pl.BlockSpec((1, TM_FFN, d), lambda b, s: (b, s, 0)),
        out_shape=jax.ShapeDtypeStruct(x.shape, F32),
        scratch_shapes=[
            pltpu.VMEM((TM_FFN, d), BF16),
            pltpu.VMEM((TM_FFN, d_ff), BF16),
        ],
        compiler_params=pltpu.CompilerParams(
            dimension_semantics=("arbitrary", "arbitrary"),
            vmem_limit_bytes=VMEM_LIMIT_BYTES),
        name="ffn_block",
    )(x, mod, g, wfi_p, wfo, fg)


def _mxu_weight(w):
    w = w.astype(BF16)
    if (w.shape[1] // LANES) % SUBLANES == 0:
        w = jnp.pad(w, ((0, 0), (0, LANES)))
    return w


def _lane_tiled(w):
    k, ch = w.shape
    return w.reshape(k, ch // LANES, LANES).transpose(1, 0, 2)


def kernel(x, c, w_ada, b_ada, norm_mix_g, w_in, conv_short_w, w_short_out, conv_conf_w, conv_conf_b, conf_ln_g, conf_ln_b, w_conf_out, w_o, norm_ffn_g, w_ffn_in, w_ffn_out, final_norm_g):
    depth = w_ada.shape[0]
    d = x.shape[-1]
    d_ff = w_ffn_out.shape[1]
    assert x.shape[1] % TM == 0 and d == 8 * LANES and d_ff % (2 * LANES) == 0
    fg = final_norm_g.reshape(1, d)
    for l in range(depth):
        mod = _adaln_mod(c, w_ada[l], b_ada[l]).reshape(c.shape[0], N_MOD, d)
        x = _mix_block(
            x, mod, norm_mix_g[l].reshape(1, d), _mxu_weight(w_in[l]), _lane_tiled(conv_short_w[l]),
            _mxu_weight(w_short_out[l]), _lane_tiled(conv_conf_w[l]), conv_conf_b[l].reshape(1, d),
            conf_ln_g[l].reshape(1, d), conf_ln_b[l].reshape(1, d), _mxu_weight(w_conf_out[l]),
            _mxu_weight(w_o[l]))
        x = _ffn_block(x, mod, norm_ffn_g[l].reshape(1, d), _mxu_weight(w_ffn_in[l]), _mxu_weight(w_ffn_out[l]),
                       fg, final_norm=(l == depth - 1))
    return x
```
